```python
import math
import jax, jax.numpy as jnp
from jax import lax
import numpy as np


D_MODEL = 1024
BATCH = 16
SEQ = 2048
DEPTH = 4

N_A_LAYERS = DEPTH // 2
N_B_LAYERS = DEPTH - N_A_LAYERS
GLA_HEADS = 4
GLA_DK = D_MODEL // 2 // GLA_HEADS
GLA_DV = D_MODEL // GLA_HEADS
GLA_KEY_WIDTH = GLA_HEADS * GLA_DK
GLA_VAL_WIDTH = GLA_HEADS * GLA_DV
GATE_RANK = 16
GATE_TAU = 16.0
GLA_CHUNK = 64
GLA_IN_WIDTH = 2 * GLA_KEY_WIDTH + GLA_VAL_WIDTH + GATE_RANK + GLA_VAL_WIDTH
DIL_GROUPS = ((128, 1), (512, 4), (2048, 16))
N_GROUPS = len(DIL_GROUPS)
B_HEADS = 8
B_HEAD_DIM = 128
B_WIDTH = B_HEADS * B_HEAD_DIM
ROT_DIM = B_HEAD_DIM // 4
ROPE_THETA = 500000.0
BAND_BLOCK = 128
D_FF = 2816
CONV_WIDTH = 3
NORM_EPS = 1e-6
NEG_INF = -1e30

kernel_name = 'yoco_gla_dilated_convffn'


def rms_norm(x, gain):
    xf = x.astype(jnp.float32)
    y = xf * lax.rsqrt(jnp.mean(xf * xf, axis=-1, keepdims=True) + NORM_EPS)
    return (y * gain.astype(jnp.float32)).astype(x.dtype)


def partial_rotary(x, positions):
    half = ROT_DIM // 2
    inv_freq = ROPE_THETA ** (-(jnp.arange(half, dtype=jnp.float32) * 2.0 / ROT_DIM))
    ang = positions.astype(jnp.float32)[..., None] * inv_freq
    ang = ang.reshape(ang.shape[:2] + (1,) * (x.ndim - 3) + (half,))
    cos, sin = jnp.cos(ang), jnp.sin(ang)
    xf = x.astype(jnp.float32)
    x1, x2, rest = xf[..., :half], xf[..., half:ROT_DIM], xf[..., ROT_DIM:]
    out = jnp.concatenate([x1 * cos - x2 * sin, x2 * cos + x1 * sin, rest], axis=-1)
    return out.astype(x.dtype)


def gla_mixer(h, w_in, w_gate, b_gate, out_norm, w_out):
    B_, S_, _ = h.shape
    proj = h @ w_in
    q, k, v, g_lr, r = jnp.split(
        proj, [GLA_KEY_WIDTH, 2 * GLA_KEY_WIDTH, 2 * GLA_KEY_WIDTH + GLA_VAL_WIDTH,
               2 * GLA_KEY_WIDTH + GLA_VAL_WIDTH + GATE_RANK], axis=-1)
    logit = (g_lr @ w_gate + b_gate).astype(jnp.float32)
    log_a = jax.nn.log_sigmoid(logit) / GATE_TAU
    n = S_ // GLA_CHUNK

    def chunks(t, d):
        return t.astype(jnp.float32).reshape(B_, n, GLA_CHUNK, GLA_HEADS, d)

    q = chunks(q, GLA_DK) * (GLA_DK ** -0.5)
    k = chunks(k, GLA_DK)
    v = chunks(v, GLA_DV)
    b = jnp.cumsum(chunks(log_a, GLA_DK), axis=2)
    b_last = b[:, :, -1:]
    q_t = q * jnp.exp(b)
    k_t = k * jnp.exp(-b)
    causal = jnp.tril(jnp.ones((GLA_CHUNK, GLA_CHUNK), dtype=bool))
    att = jnp.where(causal, jnp.einsum('bncha,bnmha->bnhcm', q_t, k_t), 0.0)
    o_intra = jnp.einsum('bnhcm,bnmhv->bnchv', att, v)
    u = jnp.einsum('bncha,bnchv->bnhav', k * jnp.exp(b_last - b), v)
    decay = jnp.exp(b_last[:, :, 0])

    def step(state, xs):
        d, uc = xs
        return d[..., None] * state + uc, state

    init = jnp.zeros((B_, GLA_HEADS, GLA_DK, GLA_DV), jnp.float32)
    _, states = lax.scan(step, init, (jnp.moveaxis(decay, 1, 0), jnp.moveaxis(u, 1, 0)))
    o_inter = jnp.einsum('bncha,nbhav->bnchv', q_t, states)
    o = (o_intra + o_inter).reshape(B_, S_, GLA_HEADS, GLA_DV)
    o = rms_norm(o, out_norm).reshape(B_, S_, GLA_VAL_WIDTH).astype(h.dtype)
    return (o * jax.nn.silu(r)) @ w_out


def dilated_window_attention(q, k, v, window, dilation):
    B_, S_, H_, hd = q.shape
    L = S_ // dilation
    w = window // dilation
    C = BAND_BLOCK
    nb = -(-L // C)
    Lp = nb * C

    def to_sub(t):
        return t.reshape(B_, L, dilation, H_, hd).transpose(0, 2, 1, 3, 4)

    qs = jnp.pad(to_sub(q), ((0, 0), (0, 0), (0, Lp - L), (0, 0), (0, 0))).reshape(B_, dilation, nb, C, H_, hd)

    def key_bands(t):
        tp = jnp.pad(to_sub(t), ((0, 0), (0, 0), (C, Lp - L), (0, 0), (0, 0)))
        tp = tp.reshape(B_, dilation, nb + 1, C, H_, hd)
        return jnp.concatenate([tp[:, :, :-1], tp[:, :, 1:]], axis=3)

    kb, vb = key_bands(k), key_bands(v)
    s = jnp.einsum('bgnqhd,bgnkhd->bgnhqk', qs, kb,
                   preferred_element_type=jnp.float32) * (hd ** -0.5)
    qi = jnp.arange(C)[:, None]
    ki = jnp.arange(2 * C)[None, :]
    dist = qi + C - ki
    key_pos = jnp.arange(nb)[:, None, None] * C - C + ki[None]
    mask = (dist >= 0)[None] & (dist <= w)[None] & (key_pos >= 0)
    s = jnp.where(mask[:, None], s, NEG_INF)
    lse = jax.nn.logsumexp(s, axis=-1)
    p = jnp.exp(s - lse[..., None])
    o = jnp.einsum('bgnhqk,bgnkhd->bgnqhd', p.astype(v.dtype), vb)
    o = o.reshape(B_, dilation, Lp, H_, hd)[:, :, :L].transpose(0, 2, 1, 3, 4).reshape(B_, S_, H_, hd)
    lse = lse.transpose(0, 1, 2, 4, 3).reshape(B_, dilation, Lp, H_)[:, :, :L]
    lse = lse.transpose(0, 2, 1, 3).reshape(B_, S_, H_)
    return o, lse


def shared_kv(x, kv_norm, w_kv, k_norm, positions):
    B_, S_, _ = x.shape
    kv = (rms_norm(x, kv_norm) @ w_kv).reshape(B_, S_, 2, N_GROUPS, B_HEADS, B_HEAD_DIM)
    k = partial_rotary(rms_norm(kv[:, :, 0], k_norm[:, None, :]), positions)
    return k, kv[:, :, 1]


def dilated_mixer(h, w_q, q_norm, w_out, k_sh, v_sh, positions):
    B_, S_, _ = h.shape
    q = (h @ w_q).reshape(B_, S_, N_GROUPS, B_HEADS, B_HEAD_DIM)
    q = partial_rotary(rms_norm(q, q_norm[:, None, :]), positions)
    outs, lses = [], []
    for g, (window, dilation) in enumerate(DIL_GROUPS):
        o_g, lse_g = dilated_window_attention(q[:, :, g], k_sh[:, :, g], v_sh[:, :, g], window, dilation)
        outs.append(o_g)
        lses.append(lse_g)
    wts = jax.nn.softmax(jnp.stack(lses, axis=0), axis=0)
    o = jnp.sum(wts[..., None] * jnp.stack(outs, axis=0).astype(jnp.float32), axis=0)
    return o.reshape(B_, S_, B_WIDTH).astype(h.dtype) @ w_out


def conv_ffn(h, w_up, conv_w, conv_b, w_down):
    u = h @ w_up
    u = lax.conv_general_dilated(u, conv_w[:, None, :].astype(u.dtype), window_strides=(1,),
                                 padding=[(CONV_WIDTH - 1, 0)],
                                 dimension_numbers=('NWC', 'WIO', 'NWC'),
                                 feature_group_count=u.shape[-1]) + conv_b
    gate, val = jnp.split(u, 2, axis=-1)
    return (jax.nn.silu(gate) * val) @ w_down


def setup_inputs(seed: int = 0) -> dict:
    key = jax.random.key(seed)
    ks = jax.random.split(key, 20)
    f32 = jnp.float32
    out_scale = (2 * DEPTH) ** -0.5

    def nrm(k, shape, scale):
        return jax.random.normal(k, shape, f32) * scale

    def gain(k, shape):
        return 1.0 + 0.05 * jax.random.normal(k, shape, f32)

    x = jax.random.normal(ks[0], (BATCH, SEQ, D_MODEL), f32)
    positions = (jax.random.randint(ks[1], (BATCH, 1), 0, 4096, dtype=jnp.int32)
                 + jnp.arange(SEQ, dtype=jnp.int32)[None, :])
    return {
        'x': x,
        'positions': positions,
        'a_norm': gain(ks[2], (N_A_LAYERS, D_MODEL)),
        'a_w_in': nrm(ks[3], (N_A_LAYERS, D_MODEL, GLA_IN_WIDTH), D_MODEL ** -0.5),
        'a_w_gate': nrm(ks[4], (N_A_LAYERS, GATE_RANK, GLA_KEY_WIDTH), GATE_RANK ** -0.5),
        'a_b_gate': nrm(ks[5], (N_A_LAYERS, GLA_KEY_WIDTH), 0.1),
        'a_out_norm': gain(ks[6], (N_A_LAYERS, GLA_DV)),
        'a_w_out': nrm(ks[7], (N_A_LAYERS, GLA_VAL_WIDTH, D_MODEL), GLA_VAL_WIDTH ** -0.5 * out_scale),
        'kv_norm': gain(ks[8], (D_MODEL,)),
        'w_kv': nrm(ks[9], (D_MODEL, 2 * N_GROUPS * B_WIDTH), D_MODEL ** -0.5),
        'k_norm': gain(ks[10], (N_GROUPS, B_HEAD_DIM)),
        'b_norm': gain(ks[11], (N_B_LAYERS, D_MODEL)),
        'b_w_q': nrm(ks[12], (N_B_LAYERS, D_MODEL, N_GROUPS * B_WIDTH), D_MODEL ** -0.5),
        'q_norm': gain(ks[13], (N_B_LAYERS, N_GROUPS, B_HEAD_DIM)),
        'b_w_out': nrm(ks[14], (N_B_LAYERS, B_WIDTH, D_MODEL), B_WIDTH ** -0.5 * out_scale),
        'f_norm': gain(ks[15], (DEPTH, D_MODEL)),
        'f_w_up': nrm(ks[16], (DEPTH, D_MODEL, 2 * D_FF), D_MODEL ** -0.5),
        'f_conv': nrm(ks[17], (DEPTH, CONV_WIDTH, 2 * D_FF), CONV_WIDTH ** -0.5),
        'f_conv_b': nrm(ks[18], (DEPTH, 2 * D_FF), 0.02),
        'f_w_down': nrm(ks[19], (DEPTH, D_FF, D_MODEL), D_FF ** -0.5 * out_scale),
    }


def reference(x, positions, a_norm, a_w_in, a_w_gate, a_b_gate, a_out_norm, a_w_out,
              kv_norm, w_kv, k_norm, b_norm, b_w_q, q_norm, b_w_out,
              f_norm, f_w_up, f_conv, f_conv_b, f_w_down):
    k_sh, v_sh = None, None
    for i in range(DEPTH):
        if i < N_A_LAYERS:
            x = x + gla_mixer(rms_norm(x, a_norm[i]), a_w_in[i], a_w_gate[i], a_b_gate[i],
                              a_out_norm[i], a_w_out[i])
        else:
            j = i - N_A_LAYERS
            if j == 0:
                k_sh, v_sh = shared_kv(x, kv_norm, w_kv, k_norm, positions)
            x = x + dilated_mixer(rms_norm(x, b_norm[j]), b_w_q[j], q_norm[j], b_w_out[j],
                                  k_sh, v_sh, positions)
        x = x + conv_ffn(rms_norm(x, f_norm[i]), f_w_up[i], f_conv[i], f_conv_b[i], f_w_down[i])
    return x
```

```python
import functools

import jax
import jax.numpy as jnp
from jax import lax
from jax.experimental import pallas as pl
from jax.experimental.pallas import tpu as pltpu

F32 = jnp.float32
BF16 = jnp.bfloat16

D_MODEL = 1024
DEPTH = 4
N_A_LAYERS = DEPTH // 2
N_B_LAYERS = DEPTH - N_A_LAYERS
GLA_HEADS = 4
GLA_DK = 128
GLA_DV = 256
GLA_KEY_WIDTH = GLA_HEADS * GLA_DK
GLA_VAL_WIDTH = GLA_HEADS * GLA_DV
GATE_RANK = 16
GATE_TAU = 16.0
GLA_CHUNK = 64
DIL_GROUPS = ((128, 1), (512, 4), (2048, 16))
N_GROUPS = len(DIL_GROUPS)
B_HEADS = 8
B_HEAD_DIM = 128
B_WIDTH = B_HEADS * B_HEAD_DIM
ROT_DIM = B_HEAD_DIM // 4
ROPE_THETA = 500000.0
BAND_BLOCK = 128
D_FF = 2816
CONV_WIDTH = 3
NORM_EPS = 1e-6
NEG_INF = -1e30

LANES = 128
SUBLANES = 8
VMEM_LIMIT = 56 * 1024 * 1024

_Q0 = 0
_K0 = GLA_KEY_WIDTH
_V0 = 2 * GLA_KEY_WIDTH
_R0 = _V0 + GLA_VAL_WIDTH
_G0 = _R0 + GLA_VAL_WIDTH
GLA_PROJ_WIDTH = _G0 + LANES

TOK_TILE = 512
CUM_BLOCK = 256
FFN_CHUNK = 256
CONV_HALO = SUBLANES

_NT = (((1,), (1,)), ((), ()))
_TN = (((0,), (0,)), ((), ()))


def _cparams(n_axes):
    return pltpu.CompilerParams(
        dimension_semantics=("arbitrary",) * n_axes,
        vmem_limit_bytes=VMEM_LIMIT)


def _rms(x, gain):
    ms = jnp.mean(x * x, axis=-1, keepdims=True)
    return x * lax.rsqrt(ms + NORM_EPS) * gain


def _silu(x):
    return x / (1.0 + jnp.exp(-x))


def _split_dot(a, b_bf16):
    hi = a.astype(BF16)
    lo = (a - hi.astype(F32)).astype(BF16)
    return (jnp.dot(hi, b_bf16, preferred_element_type=F32)
            + jnp.dot(lo, b_bf16, preferred_element_type=F32))


def _gla_layer_kernel(x_ref, gain_ref, w_in_ref, w_gate_ref, b_gate_ref,
                      onorm_ref, w_out_ref, out_ref,
                      proj_ref, b_ref, o_ref, state_ref):
    tile = x_ref.shape[1]

    @pl.when(pl.program_id(1) == 0)
    def _():
        state_ref[...] = jnp.zeros_like(state_ref)

    x = x_ref[0]
    h = _rms(x, gain_ref[...]).astype(BF16)
    proj_ref[...] = jnp.dot(h, w_in_ref[...], preferred_element_type=F32)

    g_lr = proj_ref[:, _G0:_G0 + LANES].astype(BF16)
    z = jnp.dot(g_lr, w_gate_ref[...], preferred_element_type=F32) + b_gate_ref[...]
    log_a = (jnp.minimum(z, 0.0) - jnp.log(1.0 + jnp.exp(-jnp.abs(z)))) * (1.0 / GATE_TAU)

    ri = lax.broadcasted_iota(jnp.int32, (CUM_BLOCK, CUM_BLOCK), 0)
    ci = lax.broadcasted_iota(jnp.int32, (CUM_BLOCK, CUM_BLOCK), 1)
    cum = ((ri // GLA_CHUNK == ci // GLA_CHUNK) & (ci <= ri)).astype(BF16)
    for blk in range(tile // CUM_BLOCK):
        rows = slice(blk * CUM_BLOCK, (blk + 1) * CUM_BLOCK)
        hi = log_a[rows].astype(BF16)
        lo = (log_a[rows] - hi.astype(F32)).astype(BF16)
        b_ref[rows, :] = (jnp.dot(cum, hi, preferred_element_type=F32)
                          + jnp.dot(cum, lo, preferred_element_type=F32))

    cr = lax.broadcasted_iota(jnp.int32, (GLA_CHUNK, GLA_CHUNK), 0)
    cc = lax.broadcasted_iota(jnp.int32, (GLA_CHUNK, GLA_CHUNK), 1)
    causal = cc <= cr
    scale = GLA_DK ** -0.5

    def chunk_body(c, carry):
        r0 = pl.multiple_of(c * GLA_CHUNK, GLA_CHUNK)
        rows = pl.ds(r0, GLA_CHUNK)
        for hh in range(GLA_HEADS):
            kcols = slice(hh * GLA_DK, (hh + 1) * GLA_DK)
            vcols = slice(hh * GLA_DV, (hh + 1) * GLA_DV)
            b = b_ref[rows, kcols]
            b_last = b[GLA_CHUNK - 1:GLA_CHUNK, :]
            q = proj_ref[rows, _Q0 + hh * GLA_DK:_Q0 + (hh + 1) * GLA_DK]
            k = proj_ref[rows, _K0 + hh * GLA_DK:_K0 + (hh + 1) * GLA_DK]
            v = proj_ref[rows, _V0 + hh * GLA_DV:_V0 + (hh + 1) * GLA_DV].astype(BF16)
            q_t = (q * scale * jnp.exp(b)).astype(BF16)
            k_t = (k * jnp.exp(-b)).astype(BF16)
            k_d = (k * jnp.exp(b_last - b)).astype(BF16)
            att = lax.dot_general(q_t, k_t, _NT, preferred_element_type=F32)
            att = jnp.where(causal, att, 0.0).astype(BF16)
            st = state_ref[hh]
            o = jnp.dot(att, v, preferred_element_type=F32)
            o = o + lax.dot_general(q_t, st.astype(BF16), _NT, preferred_element_type=F32)
            o_ref[rows, vcols] = o
            upd = lax.dot_general(v, k_d, _TN, preferred_element_type=F32)
            state_ref[hh] = st * jnp.exp(b_last) + upd
        return carry

    lax.fori_loop(0, tile // GLA_CHUNK, chunk_body, 0)

    onorm = onorm_ref[...]
    gated = []
    for hh in range(GLA_HEADS):
        vcols = slice(hh * GLA_DV, (hh + 1) * GLA_DV)
        o_n = _rms(o_ref[:, vcols], onorm)
        r = proj_ref[:, _R0 + hh * GLA_DV:_R0 + (hh + 1) * GLA_DV]
        gated.append((o_n * _silu(r)).astype(BF16))
    y = jnp.concatenate(gated, axis=1)
    out_ref[0] = x + jnp.dot(y, w_out_ref[...], preferred_element_type=F32)


def _gla_layer(x, gain, w_in, w_gate, b_gate, onorm, w_out):
    bsz, seq, d = x.shape
    tile = TOK_TILE
    full = lambda a: pl.BlockSpec(a.shape, lambda b, s: (0,) * a.ndim)
    return pl.pallas_call(
        _gla_layer_kernel,
        grid=(bsz, seq // tile),
        in_specs=[pl.BlockSpec((1, tile, d), lambda b, s: (b, s, 0)),
                  full(gain), full(w_in), full(w_gate), full(b_gate), full(onorm), full(w_out)],
        out_specs=pl.BlockSpec((1, tile, d), lambda b, s: (b, s, 0)),
        out_shape=jax.ShapeDtypeStruct(x.shape, F32),
        scratch_shapes=[pltpu.VMEM((tile, GLA_PROJ_WIDTH), F32),
                        pltpu.VMEM((tile, GLA_KEY_WIDTH), F32),
                        pltpu.VMEM((tile, GLA_VAL_WIDTH), F32),
                        pltpu.VMEM((GLA_HEADS, GLA_DV, GLA_DK), F32)],
        compiler_params=_cparams(2),
        name="gla_layer",
    )(x, gain, w_in, w_gate, b_gate, onorm, w_out)


def _ffn_kernel(x_ref, gain_ref, w_up_ref, conv_ref, convb_ref, w_down_ref, out_ref,
                ubuf_ref, carry_ref, acc_ref):
    tile = x_ref.shape[1]

    @pl.when(pl.program_id(1) == 0)
    def _():
        carry_ref[...] = jnp.zeros_like(carry_ref)

    x = x_ref[0]
    h = _rms(x, gain_ref[...]).astype(BF16)

    def conv_half(col0):
        cols = slice(col0, col0 + FFN_CHUNK)
        u = jnp.dot(h, w_up_ref[:, cols], preferred_element_type=F32)
        ubuf_ref[0:CONV_HALO, :] = carry_ref[:, cols]
        ubuf_ref[CONV_HALO:CONV_HALO + tile, :] = u
        carry_ref[:, cols] = u[tile - CONV_HALO:tile, :]
        taps = conv_ref[:, cols]
        y = ubuf_ref[CONV_HALO:CONV_HALO + tile, :] * taps[2:3, :]
        y = y + ubuf_ref[CONV_HALO - 1:CONV_HALO - 1 + tile, :] * taps[1:2, :]
        y = y + ubuf_ref[CONV_HALO - 2:CONV_HALO - 2 + tile, :] * taps[0:1, :]
        return y + convb_ref[:, cols]

    for j in range(D_FF // FFN_CHUNK):
        gate = conv_half(j * FFN_CHUNK)
        val = conv_half(D_FF + j * FFN_CHUNK)
        act = (_silu(gate) * val).astype(BF16)
        part = jnp.dot(act, w_down_ref[j * FFN_CHUNK:(j + 1) * FFN_CHUNK, :],
                       preferred_element_type=F32)
        if j == 0:
            acc_ref[...] = part
        else:
            acc_ref[...] += part
    out_ref[0] = x + acc_ref[...]


def _conv_ffn(x, gain, w_up, conv_w, conv_b, w_down):
    bsz, seq, d = x.shape
    tile = TOK_TILE
    full = lambda a: pl.BlockSpec(a.shape, lambda b, s: (0,) * a.ndim)
    return pl.pallas_call(
        _ffn_kernel,
        grid=(bsz, seq // tile),
        in_specs=[pl.BlockSpec((1, tile, d), lambda b, s: (b, s, 0)),
                  full(gain), full(w_up), full(conv_w), full(conv_b), full(w_down)],
        out_specs=pl.BlockSpec((1, tile, d), lambda b, s: (b, s, 0)),
        out_shape=jax.ShapeDtypeStruct(x.shape, F32),
        scratch_shapes=[pltpu.VMEM((CONV_HALO + tile, FFN_CHUNK), F32),
                        pltpu.VMEM((CONV_HALO, 2 * D_FF), F32),
                        pltpu.VMEM((tile, d), F32)],
        compiler_params=_cparams(2),
        name="conv_ffn",
    )(x, gain, w_up, conv_w, conv_b, w_down)


def _head_proj_kernel(x_ref, pos_ref, freq_ref, gain_ref, w_ref, hnorm_ref, *out_refs,
                      n_normed, out_scale):
    x = x_ref[...]
    h = _rms(x, gain_ref[...]).astype(BF16)

    lane = lax.broadcasted_iota(jnp.int32, (1, LANES), 1)
    half = ROT_DIM // 2
    ang = pos_ref[...] * freq_ref[...]
    cos, sin = jnp.cos(ang), jnp.sin(ang)
    c_same = jnp.where(lane < ROT_DIM, cos, 1.0)
    c_up = jnp.where(lane < half, -sin, 0.0)
    c_dn = jnp.where((lane >= half) & (lane < ROT_DIM), sin, 0.0)

    for g, o_ref in enumerate(out_refs):
        y = jnp.dot(h, w_ref[:, g * B_WIDTH:(g + 1) * B_WIDTH], preferred_element_type=F32)
        if g >= n_normed:
            o_ref[...] = y.astype(o_ref.dtype)
            continue
        hn = hnorm_ref[g:g + 1, :]
        for hd in range(B_HEADS):
            cols = slice(hd * B_HEAD_DIM, (hd + 1) * B_HEAD_DIM)
            yh = _rms(y[:, cols], hn)
            rot = (yh * c_same
                   + pltpu.roll(yh, LANES - half, axis=1) * c_up
                   + pltpu.roll(yh, half, axis=1) * c_dn)
            o_ref[:, cols] = (rot * out_scale).astype(o_ref.dtype)


def _head_proj(x2d, pos, freq, gain, w, hnorm, n_normed, out_scale):
    rows, d = x2d.shape
    tile = TOK_TILE
    n_out = w.shape[1] // B_WIDTH
    full = lambda a: pl.BlockSpec(a.shape, lambda i: (0,) * a.ndim)
    return pl.pallas_call(
        functools.partial(_head_proj_kernel, n_normed=n_normed, out_scale=out_scale),
        grid=(rows // tile,),
        in_specs=[pl.BlockSpec((tile, d), lambda i: (i, 0)),
                  pl.BlockSpec((tile, 1), lambda i: (i, 0)),
                  full(freq), full(gain), full(w), full(hnorm)],
        out_specs=[pl.BlockSpec((tile, B_WIDTH), lambda i: (i, 0))] * n_out,
        out_shape=[jax.ShapeDtypeStruct((rows, B_WIDTH), BF16)] * n_out,
        compiler_params=_cparams(1),
        name="head_proj",
    )(x2d, pos, freq, gain, w, hnorm)


def _band_attn_kernel(q_ref, k_ref, v_ref, o_ref, lse_ref, *, n_res, span):
    q_rows = q_ref.shape[1]
    k_rows = k_ref.shape[1]
    win = min(2 * BAND_BLOCK, k_rows)
    blocks = q_rows // BAND_BLOCK
    lane = lax.broadcasted_iota(jnp.int32, (1, LANES), 1)
    qi = lax.broadcasted_iota(jnp.int32, (BAND_BLOCK, win), 0)
    ki = lax.broadcasted_iota(jnp.int32, (BAND_BLOCK, win), 1)

    for j in range(blocks):
        nb = pl.program_id(2) * blocks + j
        k0 = jnp.clip(nb * BAND_BLOCK - BAND_BLOCK, 0, k_rows - win)
        k0 = pl.multiple_of(k0, BAND_BLOCK)
        dist = (nb * BAND_BLOCK + qi) - (k0 + ki)
        valid = (dist >= 0) & (dist <= span)
        qrows = slice(j * BAND_BLOCK, (j + 1) * BAND_BLOCK)
        for r in range(n_res):
            lse_tile = jnp.zeros((BAND_BLOCK, LANES), F32)
            for hd in range(B_HEADS):
                c0 = r * B_WIDTH + hd * B_HEAD_DIM
                cols = slice(c0, c0 + B_HEAD_DIM)
                q = q_ref[0, qrows, cols]
                k = k_ref[0, pl.ds(k0, win), cols]
                v = v_ref[0, pl.ds(k0, win), cols]
                s = lax.dot_general(q, k, _NT, preferred_element_type=F32)
                s = jnp.where(valid, s, NEG_INF)
                m = jnp.max(s, axis=1, keepdims=True)
                p = jnp.exp(s - m)
                l = jnp.sum(p, axis=1, keepdims=True)
                o = jnp.dot(p.astype(BF16), v, preferred_element_type=F32) / l
                o_ref[0, qrows, cols] = o.astype(o_ref.dtype)
                lse_tile = jnp.where(lane == hd, m + jnp.log(l), lse_tile)
            lse_ref[0, qrows, r * LANES:(r + 1) * LANES] = lse_tile


def _band_attn(q, k, v, window, dilation):
    bsz, seq, width = q.shape
    sub_len = seq // dilation
    span = window // dilation
    n_res = max(1, min(dilation, (4 * BAND_BLOCK) // sub_len))
    q_rows = min(sub_len, 4 * BAND_BLOCK)
    view = lambda a: a.reshape(bsz, sub_len, dilation * width)
    grid = (bsz, dilation // n_res, sub_len // q_rows)
    o, lse = pl.pallas_call(
        functools.partial(_band_attn_kernel, n_res=n_res, span=span),
        grid=grid,
        in_specs=[pl.BlockSpec((1, q_rows, n_res * width), lambda b, r, n: (b, n, r)),
                  pl.BlockSpec((1, sub_len, n_res * width), lambda b, r, n: (b, 0, r)),
                  pl.BlockSpec((1, sub_len, n_res * width), lambda b, r, n: (b, 0, r))],
        out_specs=[pl.BlockSpec((1, q_rows, n_res * width), lambda b, r, n: (b, n, r)),
                   pl.BlockSpec((1, q_rows, n_res * LANES), lambda b, r, n: (b, n, r))],
        out_shape=[jax.ShapeDtypeStruct((bsz, sub_len, dilation * width), BF16),
                   jax.ShapeDtypeStruct((bsz, sub_len, dilation * LANES), F32)],
        compiler_params=_cparams(3),
        name="band_attn",
    )(view(q), view(k), view(v))
    return o.reshape(bsz * seq, width), lse.reshape(bsz * seq, LANES)


def _merge_kernel(x_ref, o0_ref, o1_ref, o2_ref, l0_ref, l1_ref, l2_ref, w_out_ref, out_ref):
    lses = [l0_ref[...], l1_ref[...], l2_ref[...]]
    m = jnp.maximum(jnp.maximum(lses[0], lses[1]), lses[2])
    es = [jnp.exp(l - m) for l in lses]
    inv = 1.0 / (es[0] + es[1] + es[2])
    hr = lax.broadcasted_iota(jnp.int32, (LANES, B_WIDTH), 0)
    hc = lax.broadcasted_iota(jnp.int32, (LANES, B_WIDTH), 1)
    expand = (hc // B_HEAD_DIM == hr).astype(BF16)
    acc = None
    for e, o_ref in zip(es, (o0_ref, o1_ref, o2_ref)):
        term = _split_dot(e * inv, expand) * o_ref[...].astype(F32)
        acc = term if acc is None else acc + term
    out_ref[...] = x_ref[...] + jnp.dot(acc.astype(BF16), w_out_ref[...],
                                        preferred_element_type=F32)


def _merge(x2d, os, lses, w_out):
    rows, d = x2d.shape
    tile = TOK_TILE
    row_spec = lambda w: pl.BlockSpec((tile, w), lambda i: (i, 0))
    return pl.pallas_call(
        _merge_kernel,
        grid=(rows // tile,),
        in_specs=[row_spec(d)] + [row_spec(B_WIDTH)] * 3 + [row_spec(LANES)] * 3
                 + [pl.BlockSpec(w_out.shape, lambda i: (0, 0))],
        out_specs=row_spec(d),
        out_shape=jax.ShapeDtypeStruct(x2d.shape, F32),
        compiler_params=_cparams(1),
        name="merge_groups",
    )(x2d, *os, *lses, w_out)


def kernel(x, positions, a_norm, a_w_in, a_w_gate, a_b_gate, a_out_norm, a_w_out,
           kv_norm, w_kv, k_norm, b_norm, b_w_q, q_norm, b_w_out,
           f_norm, f_w_up, f_conv, f_conv_b, f_w_down):
    bsz, seq, d = x.shape
    rows = bsz * seq
    row = lambda a: a.reshape(1, -1)

    def ffn(x, i):
        return _conv_ffn(x, row(f_norm[i]), f_w_up[i].astype(BF16), f_conv[i],
                         row(f_conv_b[i]), f_w_down[i].astype(BF16))

    qkv_w = 2 * GLA_KEY_WIDTH + GLA_VAL_WIDTH
    for i in range(N_A_LAYERS):
        w = a_w_in[i]
        w_in = jnp.concatenate(
            [w[:, :qkv_w], w[:, qkv_w + GATE_RANK:], w[:, qkv_w:qkv_w + GATE_RANK],
             jnp.zeros((d, LANES - GATE_RANK), w.dtype)], axis=1).astype(BF16)
        w_gate = jnp.concatenate(
            [a_w_gate[i], jnp.zeros((LANES - GATE_RANK, GLA_KEY_WIDTH), F32)], axis=0).astype(BF16)
        x = _gla_layer(x, row(a_norm[i]), w_in, w_gate, row(a_b_gate[i]),
                       row(a_out_norm[i]), a_w_out[i].astype(BF16))
        x = ffn(x, i)

    half = ROT_DIM // 2
    lane = jnp.arange(LANES)
    inv_freq = ROPE_THETA ** (-((lane % half).astype(F32) * 2.0 / ROT_DIM))
    freq = inv_freq.reshape(1, LANES)
    pos = positions.astype(F32).reshape(rows, 1)
    x2d = x.reshape(rows, d)
    kv = _head_proj(x2d, pos, freq, row(kv_norm), w_kv.astype(BF16), k_norm,
                    n_normed=N_GROUPS, out_scale=1.0)
    ks = [a.reshape(bsz, seq, B_WIDTH) for a in kv[:N_GROUPS]]
    vs = [a.reshape(bsz, seq, B_WIDTH) for a in kv[N_GROUPS:]]
    for j in range(N_B_LAYERS):
        qs = _head_proj(x2d, pos, freq, row(b_norm[j]), b_w_q[j].astype(BF16), q_norm[j],
                        n_normed=N_GROUPS, out_scale=B_HEAD_DIM ** -0.5)
        os, lses = [], []
        for g, (window, dilation) in enumerate(DIL_GROUPS):
            o_g, lse_g = _band_attn(qs[g].reshape(bsz, seq, B_WIDTH), ks[g], vs[g],
                                    window, dilation)
            os.append(o_g)
            lses.append(lse_g)
        x2d = _merge(x2d, os, lses, b_w_out[j].astype(BF16))
        x2d = ffn(x2d.reshape(bsz, seq, d), N_A_LAYERS + j).reshape(rows, d)
    return x2d.reshape(bsz, seq, d)
```

```python
import functools

import jax
import jax.numpy as jnp
from jax import lax
from jax.experimental import pallas as pl
from jax.experimental.pallas import tpu as pltpu

F32 = jnp.float32
BF16 = jnp.bfloat16

D_MODEL = 1024
DEPTH = 4
N_A_LAYERS = DEPTH // 2
N_B_LAYERS = DEPTH - N_A_LAYERS
GLA_HEADS = 4
GLA_DK = 128
GLA_DV = 256
GLA_KEY_WIDTH = GLA_HEADS * GLA_DK
GLA_VAL_WIDTH = GLA_HEADS * GLA_DV
GATE_RANK = 16
GATE_TAU = 16.0
GLA_CHUNK = 64
DIL_GROUPS = ((128, 1), (512, 4), (2048, 16))
N_GROUPS = len(DIL_GROUPS)
B_HEADS = 8
B_HEAD_DIM = 128
B_WIDTH = B_HEADS * B_HEAD_DIM
ROT_DIM = B_HEAD_DIM // 4
ROPE_THETA = 500000.0
BAND_BLOCK = 128
D_FF = 2816
CONV_WIDTH = 3
NORM_EPS = 1e-6
NEG_INF = -1e30

LANES = 128
SUBLANES = 8
VMEM_LIMIT = 56 * 1024 * 1024

_Q0 = 0
_K0 = GLA_KEY_WIDTH
_V0 = 2 * GLA_KEY_WIDTH
_R0 = _V0 + GLA_VAL_WIDTH
_G0 = _R0 + GLA_VAL_WIDTH
GLA_PROJ_WIDTH = _G0 + LANES

TOK_TILE = 512
ATTN_ROWS = 512
CUM_BLOCK = 256
FFN_CHUNK = 256
CONV_HALO = SUBLANES
UBUF_SLOTS = 4

_NT = (((1,), (1,)), ((), ()))
_TN = (((0,), (0,)), ((), ()))


def _cparams(n_axes):
    return pltpu.CompilerParams(
        dimension_semantics=("arbitrary",) * n_axes,
        vmem_limit_bytes=VMEM_LIMIT)


def _rms(x, gain):
    ms = jnp.mean(x * x, axis=-1, keepdims=True)
    return x * lax.rsqrt(ms + NORM_EPS) * gain


def _silu(x):
    return x / (1.0 + jnp.exp(-x))


def _gla_layer_kernel(x_ref, gain_ref, w_in_ref, w_gate_ref, b_gate_ref,
                      onorm_ref, w_out_ref, out_ref,
                      proj_ref, b_ref, o_ref, state_ref):
    tile = x_ref.shape[1]

    @pl.when(pl.program_id(1) == 0)
    def _():
        state_ref[...] = jnp.zeros_like(state_ref)

    x = x_ref[0]
    h = _rms(x, gain_ref[...]).astype(BF16)
    proj_ref[...] = jnp.dot(h, w_in_ref[...], preferred_element_type=F32)

    g_lr = proj_ref[:, _G0:_G0 + LANES].astype(BF16)
    z = jnp.dot(g_lr, w_gate_ref[...], preferred_element_type=F32) + b_gate_ref[...]
    log_a = (jnp.minimum(z, 0.0) - jnp.log(1.0 + jnp.exp(-jnp.abs(z)))) * (1.0 / GATE_TAU)

    ri = lax.broadcasted_iota(jnp.int32, (CUM_BLOCK, CUM_BLOCK), 0)
    ci = lax.broadcasted_iota(jnp.int32, (CUM_BLOCK, CUM_BLOCK), 1)
    cum = ((ri // GLA_CHUNK == ci // GLA_CHUNK) & (ci <= ri)).astype(BF16)
    for blk in range(tile // CUM_BLOCK):
        rows = slice(blk * CUM_BLOCK, (blk + 1) * CUM_BLOCK)
        hi = log_a[rows].astype(BF16)
        lo = (log_a[rows] - hi.astype(F32)).astype(BF16)
        b_ref[rows, :] = (jnp.dot(cum, hi, preferred_element_type=F32)
                          + jnp.dot(cum, lo, preferred_element_type=F32))

    cr = lax.broadcasted_iota(jnp.int32, (GLA_CHUNK, GLA_CHUNK), 0)
    cc = lax.broadcasted_iota(jnp.int32, (GLA_CHUNK, GLA_CHUNK), 1)
    causal = cc <= cr
    scale = GLA_DK ** -0.5

    def chunk_body(c, carry):
        r0 = pl.multiple_of(c * GLA_CHUNK, GLA_CHUNK)
        rows = pl.ds(r0, GLA_CHUNK)
        for hh in range(GLA_HEADS):
            kcols = slice(hh * GLA_DK, (hh + 1) * GLA_DK)
            vcols = slice(hh * GLA_DV, (hh + 1) * GLA_DV)
            b = b_ref[rows, kcols]
            b_last = b[GLA_CHUNK - 1:GLA_CHUNK, :]
            q = proj_ref[rows, _Q0 + hh * GLA_DK:_Q0 + (hh + 1) * GLA_DK]
            k = proj_ref[rows, _K0 + hh * GLA_DK:_K0 + (hh + 1) * GLA_DK]
            v = proj_ref[rows, _V0 + hh * GLA_DV:_V0 + (hh + 1) * GLA_DV].astype(BF16)
            q_t = (q * scale * jnp.exp(b)).astype(BF16)
            k_t = (k * jnp.exp(-b)).astype(BF16)
            k_d = (k * jnp.exp(b_last - b)).astype(BF16)
            att = lax.dot_general(q_t, k_t, _NT, preferred_element_type=F32)
            att = jnp.where(causal, att, 0.0).astype(BF16)
            st = state_ref[hh]
            o = jnp.dot(att, v, preferred_element_type=F32)
            o = o + lax.dot_general(q_t, st.astype(BF16), _NT, preferred_element_type=F32)
            o_ref[rows, vcols] = o
            upd = lax.dot_general(v, k_d, _TN, preferred_element_type=F32)
            state_ref[hh] = st * jnp.exp(b_last) + upd
        return carry

    lax.fori_loop(0, tile // GLA_CHUNK, chunk_body, 0)

    onorm = onorm_ref[...]
    gated = []
    for hh in range(GLA_HEADS):
        vcols = slice(hh * GLA_DV, (hh + 1) * GLA_DV)
        o_n = _rms(o_ref[:, vcols], onorm)
        r = proj_ref[:, _R0 + hh * GLA_DV:_R0 + (hh + 1) * GLA_DV]
        gated.append((o_n * _silu(r)).astype(BF16))
    y = jnp.concatenate(gated, axis=1)
    out_ref[0] = x + jnp.dot(y, w_out_ref[...], preferred_element_type=F32)


def _gla_layer(x, gain, w_in, w_gate, b_gate, onorm, w_out):
    bsz, seq, d = x.shape
    tile = TOK_TILE
    full = lambda a: pl.BlockSpec(a.shape, lambda b, s: (0,) * a.ndim)
    return pl.pallas_call(
        _gla_layer_kernel,
        grid=(bsz, seq // tile),
        in_specs=[pl.BlockSpec((1, tile, d), lambda b, s: (b, s, 0)),
                  full(gain), full(w_in), full(w_gate), full(b_gate), full(onorm), full(w_out)],
        out_specs=pl.BlockSpec((1, tile, d), lambda b, s: (b, s, 0)),
        out_shape=jax.ShapeDtypeStruct(x.shape, F32),
        scratch_shapes=[pltpu.VMEM((tile, GLA_PROJ_WIDTH), F32),
                        pltpu.VMEM((tile, GLA_KEY_WIDTH), F32),
                        pltpu.VMEM((tile, GLA_VAL_WIDTH), F32),
                        pltpu.VMEM((GLA_HEADS, GLA_DV, GLA_DK), F32)],
        compiler_params=_cparams(2),
        name="gla_layer",
    )(x, gain, w_in, w_gate, b_gate, onorm, w_out)


def _ffn_kernel(x_ref, gain_ref, w_up_ref, conv_ref, convb_ref, w_down_ref, out_ref,
                ubuf_ref, carry_ref, acc_ref, stage_ref):
    tile = x_ref.shape[1]
    half_rows = tile // 2
    lane_tiles = FFN_CHUNK // LANES

    @pl.when(pl.program_id(1) == 0)
    def _():
        carry_ref[...] = jnp.zeros_like(carry_ref)

    x = x_ref[0]
    h = _rms(x, gain_ref[...]).astype(BF16)

    def up_half(col0, slot):
        u = jnp.dot(h, w_up_ref[:, col0:col0 + FFN_CHUNK], preferred_element_type=F32)
        for c in range(lane_tiles):
            cols = slice(col0 + c * LANES, col0 + (c + 1) * LANES)
            uc = u[:, c * LANES:(c + 1) * LANES]
            ubuf_ref[slot, c, 0:CONV_HALO, :] = carry_ref[:, cols]
            ubuf_ref[slot, c, CONV_HALO:CONV_HALO + tile, :] = uc
            carry_ref[:, cols] = uc[tile - CONV_HALO:tile, :]

    def conv_half(col0, slot):
        outs = []
        for c in range(lane_tiles):
            cols = slice(col0 + c * LANES, col0 + (c + 1) * LANES)
            taps, bias = conv_ref[:, cols], convb_ref[:, cols]
            tap = lambda off: ubuf_ref[slot, c, pl.ds(CONV_HALO + off, half_rows, stride=2), :]
            u_m2, u_m1, u_0, u_p1 = tap(-2), tap(-1), tap(0), tap(1)
            even = u_0 * taps[2:3, :] + u_m1 * taps[1:2, :] + u_m2 * taps[0:1, :] + bias
            odd = u_p1 * taps[2:3, :] + u_0 * taps[1:2, :] + u_m1 * taps[0:1, :] + bias
            outs.append(jnp.concatenate([even, odd], axis=0))
        return jnp.concatenate(outs, axis=1)

    def up_chunk(j):
        up_half(j * FFN_CHUNK, (2 * j) % UBUF_SLOTS)
        up_half(D_FF + j * FFN_CHUNK, (2 * j + 1) % UBUF_SLOTS)

    n_chunks = D_FF // FFN_CHUNK
    up_chunk(0)
    for j in range(n_chunks):
        if j + 1 < n_chunks:
            up_chunk(j + 1)
        gate = conv_half(j * FFN_CHUNK, (2 * j) % UBUF_SLOTS)
        val = conv_half(D_FF + j * FFN_CHUNK, (2 * j + 1) % UBUF_SLOTS)
        act = (_silu(gate) * val).astype(BF16)
        part = jnp.dot(act, w_down_ref[j * FFN_CHUNK:(j + 1) * FFN_CHUNK, :],
                       preferred_element_type=F32)
        if j == 0:
            acc_ref[...] = part
        else:
            acc_ref[...] += part
    for c in range(x.shape[1] // LANES):
        cols = slice(c * LANES, (c + 1) * LANES)
        stage_ref[c, pl.ds(0, half_rows, stride=2), :] = acc_ref[0:half_rows, cols]
        stage_ref[c, pl.ds(1, half_rows, stride=2), :] = acc_ref[half_rows:tile, cols]
        out_ref[0, :, cols] = x[:, cols] + stage_ref[c]


def _conv_ffn(x, gain, w_up, conv_w, conv_b, w_down):
    bsz, seq, d = x.shape
    tile = TOK_TILE
    full = lambda a: pl.BlockSpec(a.shape, lambda b, s: (0,) * a.ndim)
    return pl.pallas_call(
        _ffn_kernel,
        grid=(bsz, seq // tile),
        in_specs=[pl.BlockSpec((1, tile, d), lambda b, s: (b, s, 0)),
                  full(gain), full(w_up), full(conv_w), full(conv_b), full(w_down)],
        out_specs=pl.BlockSpec((1, tile, d), lambda b, s: (b, s, 0)),
        out_shape=jax.ShapeDtypeStruct(x.shape, F32),
        scratch_shapes=[pltpu.VMEM((UBUF_SLOTS, FFN_CHUNK // LANES, CONV_HALO + tile, LANES), F32),
                        pltpu.VMEM((CONV_HALO, 2 * D_FF), F32),
                        pltpu.VMEM((tile, d), F32),
                        pltpu.VMEM((d // LANES, tile, LANES), F32)],
        compiler_params=_cparams(2),
        name="conv_ffn",
    )(x, gain, w_up, conv_w, conv_b, w_down)


def _store_by_residue(o_ref, stage_ref, hd, val):
    dil, sub = o_ref.shape[1], o_ref.shape[2]
    cols = slice(hd * LANES, (hd + 1) * LANES)
    if dil == 1:
        o_ref[0, 0, :, cols] = val.astype(o_ref.dtype)
        return
    stage_ref[hd] = val
    for r in range(dil):
        o_ref[0, r, :, cols] = stage_ref[hd, pl.ds(r, sub, stride=dil), :].astype(o_ref.dtype)


def _head_proj_kernel(x_ref, pos_ref, freq_ref, gain_ref, w_ref, hnorm_ref, *refs,
                      n_normed, out_scale):
    out_refs, stage_ref = refs[:-1], refs[-1]
    x = x_ref[0]
    h = _rms(x, gain_ref[...]).astype(BF16)

    lane = lax.broadcasted_iota(jnp.int32, (1, LANES), 1)
    half = ROT_DIM // 2
    ang = pos_ref[0] * freq_ref[...]
    cos, sin = jnp.cos(ang), jnp.sin(ang)
    c_same = jnp.where(lane < ROT_DIM, cos, 1.0)
    c_up = jnp.where(lane < half, -sin, 0.0)
    c_dn = jnp.where((lane >= half) & (lane < ROT_DIM), sin, 0.0)

    for g, o_ref in enumerate(out_refs):
        y = jnp.dot(h, w_ref[:, g * B_WIDTH:(g + 1) * B_WIDTH], preferred_element_type=F32)
        for hd in range(B_HEADS):
            yh = y[:, hd * B_HEAD_DIM:(hd + 1) * B_HEAD_DIM]
            if g < n_normed:
                yh = _rms(yh, hnorm_ref[g:g + 1, :])
                yh = (yh * c_same
                      + pltpu.roll(yh, LANES - half, axis=1) * c_up
                      + pltpu.roll(yh, half, axis=1) * c_dn) * out_scale
            _store_by_residue(o_ref, stage_ref, hd, yh)


def _head_proj(x, pos, freq, gain, w, hnorm, n_normed, out_scale):
    bsz, seq, d = x.shape
    tile = TOK_TILE
    n_out = w.shape[1] // B_WIDTH
    dils = [DIL_GROUPS[g % N_GROUPS][1] for g in range(n_out)]
    full = lambda a: pl.BlockSpec(a.shape, lambda b, s: (0,) * a.ndim)
    return pl.pallas_call(
        functools.partial(_head_proj_kernel, n_normed=n_normed, out_scale=out_scale),
        grid=(bsz, seq // tile),
        in_specs=[pl.BlockSpec((1, tile, d), lambda b, s: (b, s, 0)),
                  pl.BlockSpec((1, tile, 1), lambda b, s: (b, s, 0)),
                  full(freq), full(gain), full(w), full(hnorm)],
        out_specs=[pl.BlockSpec((1, dil, tile // dil, B_WIDTH), lambda b, s: (b, 0, s, 0))
                   for dil in dils],
        out_shape=[jax.ShapeDtypeStruct((bsz, dil, seq // dil, B_WIDTH), BF16) for dil in dils],
        scratch_shapes=[pltpu.VMEM((B_HEADS, tile, LANES), F32)],
        compiler_params=_cparams(2),
        name="head_proj",
    )(x, pos, freq, gain, w, hnorm)


def _band_attn_kernel(*refs, n_res, span, has_prev):
    if has_prev:
        q_ref, kp_ref, k_ref, vp_ref, v_ref, o_ref, lse_ref = refs
    else:
        q_ref, k_ref, v_ref, o_ref, lse_ref = refs
    blocks = q_ref.shape[2] // BAND_BLOCK
    lane = lax.broadcasted_iota(jnp.int32, (1, LANES), 1)

    for j in range(blocks):
        win = BAND_BLOCK if (j == 0 and not has_prev) else 2 * BAND_BLOCK
        qi = lax.broadcasted_iota(jnp.int32, (BAND_BLOCK, win), 0)
        ki = lax.broadcasted_iota(jnp.int32, (BAND_BLOCK, win), 1)
        dist = qi + (win - BAND_BLOCK) - ki
        valid = (dist >= 0) & (dist <= span)
        if j == 0 and has_prev:
            valid = valid & ((ki >= BAND_BLOCK) | (pl.program_id(2) > 0))
        qrows = slice(j * BAND_BLOCK, (j + 1) * BAND_BLOCK)
        krows = slice(j * BAND_BLOCK + BAND_BLOCK - win, (j + 1) * BAND_BLOCK)
        head_cols = [slice(hd * B_HEAD_DIM, (hd + 1) * B_HEAD_DIM) for hd in range(B_HEADS)]

        def keys(ref, prev_ref, r, cols):
            if j == 0 and has_prev:
                return jnp.concatenate([prev_ref[0, r, :, cols], ref[0, r, 0:BAND_BLOCK, cols]], axis=0)
            return ref[0, r, krows, cols]

        for r in range(n_res):
            scores = [lax.dot_general(q_ref[0, r, qrows, cols],
                                      keys(k_ref, kp_ref if has_prev else None, r, cols),
                                      _NT, preferred_element_type=F32) for cols in head_cols]
            probs, sums, lse_tile = [], [], jnp.zeros((BAND_BLOCK, LANES), F32)
            for hd, s in enumerate(scores):
                s = jnp.where(valid, s, NEG_INF)
                m = jnp.max(s, axis=1, keepdims=True)
                p = jnp.exp(s - m)
                l = jnp.sum(p, axis=1, keepdims=True)
                probs.append(p.astype(BF16))
                sums.append(l)
                lse_tile = jnp.where(lane == hd, m + jnp.log(l), lse_tile)
            lse_ref[0, r, qrows, :] = lse_tile
            for hd, cols in enumerate(head_cols):
                v = keys(v_ref, vp_ref if has_prev else None, r, cols)
                o = jnp.dot(probs[hd], v, preferred_element_type=F32) / sums[hd]
                o_ref[0, r, qrows, cols] = o.astype(o_ref.dtype)


def _band_attn(q, k, v, window):
    bsz, dilation, sub_len, width = q.shape
    span = window // dilation
    q_rows = min(sub_len, ATTN_ROWS)
    n_res = max(1, min(dilation, ATTN_ROWS // sub_len))
    steps = sub_len // q_rows
    has_prev = steps > 1
    per_step = q_rows // BAND_BLOCK
    q_spec = lambda w: pl.BlockSpec((1, n_res, q_rows, w), lambda b, r, n: (b, r, n, 0))
    prev_spec = pl.BlockSpec((1, n_res, BAND_BLOCK, width),
                             lambda b, r, n: (b, r, jnp.maximum(n * per_step - 1, 0), 0))
    kv_specs = [prev_spec, q_spec(width)] if has_prev else [q_spec(width)]
    kv_args = lambda a: (a, a) if has_prev else (a,)
    return pl.pallas_call(
        functools.partial(_band_attn_kernel, n_res=n_res, span=span, has_prev=has_prev),
        grid=(bsz, dilation // n_res, steps),
        in_specs=[q_spec(width)] + kv_specs + kv_specs,
        out_specs=[q_spec(width), q_spec(LANES)],
        out_shape=[jax.ShapeDtypeStruct(q.shape, BF16),
                   jax.ShapeDtypeStruct((bsz, dilation, sub_len, LANES), F32)],
        compiler_params=_cparams(3),
        name="band_attn",
    )(q, *kv_args(k), *kv_args(v))


def _load_by_residue(ref, stage_ref, slot, col_tile=None):
    dil, sub = ref.shape[1], ref.shape[2]
    cols = slice(None) if col_tile is None else slice(col_tile * LANES, (col_tile + 1) * LANES)
    if dil == 1:
        return ref[0, 0, :, cols].astype(F32)
    for r in range(dil):
        stage_ref[slot, pl.ds(r, sub, stride=dil), :] = ref[0, r, :, cols].astype(F32)
    return stage_ref[slot]


def _merge_kernel(x_ref, o0_ref, o1_ref, o2_ref, l0_ref, l1_ref, l2_ref, w_out_ref, out_ref,
                  lstage_ref, ostage_ref, acc_ref):
    o_refs = (o0_ref, o1_ref, o2_ref)
    lses = [_load_by_residue(l_ref, lstage_ref, g)
            for g, l_ref in enumerate((l0_ref, l1_ref, l2_ref))]
    m = jnp.maximum(jnp.maximum(lses[0], lses[1]), lses[2])
    es = [jnp.exp(l - m) for l in lses]
    inv = 1.0 / (es[0] + es[1] + es[2])
    hr = lax.broadcasted_iota(jnp.int32, (2 * LANES, B_WIDTH), 0)
    hc = lax.broadcasted_iota(jnp.int32, (2 * LANES, B_WIDTH), 1)
    expand = (hc // B_HEAD_DIM == hr % LANES).astype(BF16)
    wts = []
    for e in es:
        w = e * inv
        hi = w.astype(BF16)
        lo = (w - hi.astype(F32)).astype(BF16)
        wts.append(jnp.dot(jnp.concatenate([hi, lo], axis=1), expand,
                           preferred_element_type=F32))
    for hd in range(B_HEADS):
        cols = slice(hd * B_HEAD_DIM, (hd + 1) * B_HEAD_DIM)
        acc = None
        for g, o_ref in enumerate(o_refs):
            term = wts[g][:, cols] * _load_by_residue(o_ref, ostage_ref, g * B_HEADS + hd, hd)
            acc = term if acc is None else acc + term
        acc_ref[:, cols] = acc.astype(BF16)
    out_ref[0] = x_ref[0] + jnp.dot(acc_ref[...], w_out_ref[...], preferred_element_type=F32)


def _merge(x, os, lses, w_out):
    bsz, seq, d = x.shape
    tile = TOK_TILE
    tok_spec = pl.BlockSpec((1, tile, d), lambda b, s: (b, s, 0))
    res_spec = lambda a: pl.BlockSpec((1, a.shape[1], tile // a.shape[1], a.shape[3]),
                                      lambda b, s: (b, 0, s, 0))
    return pl.pallas_call(
        _merge_kernel,
        grid=(bsz, seq // tile),
        in_specs=[tok_spec] + [res_spec(a) for a in os] + [res_spec(a) for a in lses]
                 + [pl.BlockSpec(w_out.shape, lambda b, s: (0, 0))],
        out_specs=tok_spec,
        out_shape=jax.ShapeDtypeStruct(x.shape, F32),
        scratch_shapes=[pltpu.VMEM((N_GROUPS, tile, LANES), F32),
                        pltpu.VMEM((N_GROUPS * B_HEADS, tile, LANES), F32),
                        pltpu.VMEM((tile, B_WIDTH), BF16)],
        compiler_params=_cparams(2),
        name="merge_groups",
    )(x, *os, *lses, w_out)


def kernel(x, positions, a_norm, a_w_in, a_w_gate, a_b_gate, a_out_norm, a_w_out,
           kv_norm, w_kv, k_norm, b_norm, b_w_q, q_norm, b_w_out,
           f_norm, f_w_up, f_conv, f_conv_b, f_w_down):
    bsz, seq, d = x.shape
    row = lambda a: a.reshape(1, -1)

    def ffn(x, i):
        return _conv_ffn(x, row(f_norm[i]), f_w_up[i].astype(BF16), f_conv[i],
                         row(f_conv_b[i]), f_w_down[i].astype(BF16))

    qkv_w = 2 * GLA_KEY_WIDTH + GLA_VAL_WIDTH
    for i in range(N_A_LAYERS):
        w = a_w_in[i]
        w_in = jnp.concatenate(
            [w[:, :qkv_w], w[:, qkv_w + GATE_RANK:], w[:, qkv_w:qkv_w + GATE_RANK],
             jnp.zeros((d, LANES - GATE_RANK), w.dtype)], axis=1).astype(BF16)
        w_gate = jnp.concatenate(
            [a_w_gate[i], jnp.zeros((LANES - GATE_RANK, GLA_KEY_WIDTH), F32)], axis=0).astype(BF16)
        x = _gla_layer(x, row(a_norm[i]), w_in, w_gate, row(a_b_gate[i]),
                       row(a_out_norm[i]), a_w_out[i].astype(BF16))
        x = ffn(x, i)

    half = ROT_DIM // 2
    lane = jnp.arange(LANES)
    inv_freq = ROPE_THETA ** (-((lane % half).astype(F32) * 2.0 / ROT_DIM))
    freq = inv_freq.reshape(1, LANES)
    pos = positions.astype(F32).reshape(bsz, seq, 1)
    kv = _head_proj(x, pos, freq, row(kv_norm), w_kv.astype(BF16), k_norm,
                    n_normed=N_GROUPS, out_scale=1.0)
    ks, vs = kv[:N_GROUPS], kv[N_GROUPS:]
    for j in range(N_B_LAYERS):
        qs = _head_proj(x, pos, freq, row(b_norm[j]), b_w_q[j].astype(BF16), q_norm[j],
                        n_normed=N_GROUPS, out_scale=B_HEAD_DIM ** -0.5)
        os, lses = [], []
        for g, (window, _) in enumerate(DIL_GROUPS):
            o_g, lse_g = _band_attn(qs[g], ks[g], vs[g], window)
            os.append(o_g)
            lses.append(lse_g)
        x = _merge(x, os, lses, b_w_out[j].astype(BF16))
        x = ffn(x, N_A_LAYERS + j)
    return x
```
